```python
import jax, jax.numpy as jnp
from jax import lax
import numpy as np

D_MODEL = 1024
BATCH = 1
SEQ = 16384
DEPTH = 4
DEC_BATCH = 32
DEC_SEQ = 2048
PAST_LEN = 128

HEAD_DIM = D_MODEL // 16
N_HEADS = 8
N_KV_HEADS = 2
GQA_GROUP = N_HEADS // N_KV_HEADS
WINDOW = 128
BLOCK = 128
N_FOURIER_GROUPS = 4
FOURIER_GROUP_WIDTH = D_MODEL // 16
N_MEM = 256
N_MEM_HEADS = 4
ATTN_WIDTH = N_HEADS * HEAD_DIM
KV_WIDTH = N_KV_HEADS * HEAD_DIM
FOURIER_WIDTH = N_FOURIER_GROUPS * FOURIER_GROUP_WIDTH
MEM_WIDTH = N_MEM_HEADS * HEAD_DIM
MIX_WIDTH = ATTN_WIDTH + FOURIER_WIDTH + MEM_WIDTH
IN_WIDTH = ATTN_WIDTH + 2 * KV_WIDTH + FOURIER_WIDTH + MEM_WIDTH
D_FF = 4 * D_MODEL
EPS = 1e-6
NEG_INF = -1e30

kernel_name = "hymba_style_window_gqa_fnet_memory_encoder"


def rmsnorm(x, g):
    xf = x.astype(jnp.float32)
    y = xf * lax.rsqrt(jnp.mean(xf * xf, axis=-1, keepdims=True) + EPS)
    return (y * g.astype(jnp.float32)).astype(x.dtype)


def alibi_slopes():
    return jnp.exp2(-8.0 * jnp.arange(1, N_HEADS + 1, dtype=jnp.float32) / N_HEADS)


def window_gqa_attention(q, k, v, sinks):
    B, S, _ = q.shape
    nb = S // BLOCK
    qb = q.reshape(B, nb, BLOCK, N_KV_HEADS, GQA_GROUP, HEAD_DIM)
    k = k.reshape(B, S, N_KV_HEADS, HEAD_DIM)
    v = v.reshape(B, S, N_KV_HEADS, HEAD_DIM)
    pad = ((0, 0), (BLOCK, BLOCK), (0, 0), (0, 0))
    kp = jnp.pad(k, pad)
    vp = jnp.pad(v, pad)
    kb = jnp.concatenate([kp[:, i * BLOCK:i * BLOCK + S].reshape(B, nb, BLOCK, N_KV_HEADS, HEAD_DIM) for i in range(3)], axis=2)
    vb = jnp.concatenate([vp[:, i * BLOCK:i * BLOCK + S].reshape(B, nb, BLOCK, N_KV_HEADS, HEAD_DIM) for i in range(3)], axis=2)
    scale = HEAD_DIM ** -0.5
    scores = jnp.einsum('bnqkgd,bnckd->bnkgqc', qb, kb).astype(jnp.float32) * scale
    a = jnp.arange(BLOCK)
    c = jnp.arange(3 * BLOCK)
    rel = c[None, :] - BLOCK - a[:, None]
    dist = jnp.abs(rel).astype(jnp.float32)
    key_pos = (jnp.arange(nb)[:, None] - 1) * BLOCK + c[None, :]
    in_range = (key_pos >= 0) & (key_pos < S)
    mask = (jnp.abs(rel) <= WINDOW)[None, :, :] & in_range[:, None, :]
    bias = -alibi_slopes().reshape(N_KV_HEADS, GQA_GROUP, 1, 1) * dist
    scores = jnp.where(mask[None, :, None, None, :, :], scores + bias, NEG_INF)
    sink = jnp.broadcast_to(sinks.astype(jnp.float32).reshape(1, 1, N_KV_HEADS, GQA_GROUP, 1, 1), scores.shape[:-1] + (1,))
    probs = jax.nn.softmax(jnp.concatenate([scores, sink], axis=-1), axis=-1)[..., :-1]
    out = jnp.einsum('bnkgqc,bnckd->bnqkgd', probs.astype(v.dtype), vb)
    return out.reshape(B, S, ATTN_WIDTH)


def fourier_mix(u):
    B, S, _ = u.shape
    uf = u.astype(jnp.float32).reshape(B, S, N_FOURIER_GROUPS, FOURIER_GROUP_WIDTH)
    f = jnp.fft.fftn(uf, axes=(1, 3), norm='ortho').real
    return f.reshape(B, S, FOURIER_WIDTH).astype(u.dtype)


def memory_cross_attention(qm, mem_n, w_mem_kv):
    B, S, _ = qm.shape
    kv = mem_n @ w_mem_kv
    km, vm = jnp.split(kv, 2, axis=-1)
    km = km.reshape(B, N_MEM, N_MEM_HEADS, HEAD_DIM)
    vm = vm.reshape(B, N_MEM, N_MEM_HEADS, HEAD_DIM)
    q = qm.reshape(B, S, N_MEM_HEADS, HEAD_DIM)
    scores = jnp.einsum('bshd,bmhd->bhsm', q, km).astype(jnp.float32) * (HEAD_DIM ** -0.5)
    probs = jax.nn.softmax(scores, axis=-1)
    out = jnp.einsum('bhsm,bmhd->bshd', probs.astype(vm.dtype), vm)
    return out.reshape(B, S, MEM_WIDTH)


def group_output_norm(y_attn, y_four, y_mem, g):
    g_a, g_f, g_m = jnp.split(g, [ATTN_WIDTH, ATTN_WIDTH + FOURIER_WIDTH])
    return jnp.concatenate([rmsnorm(y_attn, g_a), rmsnorm(y_four, g_f), rmsnorm(y_mem, g_m)], axis=-1)


def encoder_layer(x, mem, g_mix, w_in, g_mem, w_mem_kv, sinks, g_grp, w_out, g_ffn, w_ff1, w_ff2):
    h = rmsnorm(x, g_mix)
    proj = h @ w_in
    o1 = ATTN_WIDTH
    o2 = o1 + KV_WIDTH
    o3 = o2 + KV_WIDTH
    o4 = o3 + FOURIER_WIDTH
    q, k, v, uf, qm = jnp.split(proj, [o1, o2, o3, o4], axis=-1)
    y_attn = window_gqa_attention(q, k, v, sinks)
    y_four = fourier_mix(uf)
    y_mem = memory_cross_attention(qm, rmsnorm(mem, g_mem), w_mem_kv)
    x = x + group_output_norm(y_attn, y_four, y_mem, g_grp) @ w_out
    hf = rmsnorm(x, g_ffn) @ w_ff1
    x = x + jnp.square(jax.nn.relu(hf)) @ w_ff2
    return x


def trunk(x, mem, g_mix, w_in, g_mem, w_mem_kv, sinks, g_grp, w_out, g_ffn, w_ff1, w_ff2, g_final):
    for l in range(DEPTH):
        x = encoder_layer(x, mem, g_mix[l], w_in[l], g_mem[l], w_mem_kv[l], sinks[l], g_grp[l], w_out[l], g_ffn[l], w_ff1[l], w_ff2[l])
    return rmsnorm(x, g_final)


def setup_inputs(seed: int = 0) -> dict:
    key = jax.random.key(seed)
    ks = jax.random.split(key, 16)
    f32 = jnp.float32

    def gain(k, shape):
        return 1.0 + 0.02 * jax.random.normal(k, shape, f32)

    return {
        'x_prompt': jax.random.normal(ks[0], (BATCH, SEQ, D_MODEL), f32),
        'x_sample': jax.random.normal(ks[1], (DEC_BATCH, DEC_SEQ, D_MODEL), f32),
        'mem_prompt': jax.random.normal(ks[2], (BATCH, N_MEM, D_MODEL), f32),
        'mem_sample': jax.random.normal(ks[3], (DEC_BATCH, N_MEM, D_MODEL), f32),
        'g_mix': gain(ks[4], (DEPTH, D_MODEL)),
        'w_in': jax.random.normal(ks[5], (DEPTH, D_MODEL, IN_WIDTH), f32) * D_MODEL ** -0.5,
        'g_mem': gain(ks[6], (DEPTH, D_MODEL)),
        'w_mem_kv': jax.random.normal(ks[7], (DEPTH, D_MODEL, 2 * MEM_WIDTH), f32) * D_MODEL ** -0.5,
        'sinks': 0.5 * jax.random.normal(ks[8], (DEPTH, N_HEADS), f32),
        'g_grp': gain(ks[9], (DEPTH, MIX_WIDTH)),
        'w_out': jax.random.normal(ks[10], (DEPTH, MIX_WIDTH, D_MODEL), f32) * MIX_WIDTH ** -0.5,
        'g_ffn': gain(ks[11], (DEPTH, D_MODEL)),
        'w_ff1': jax.random.normal(ks[12], (DEPTH, D_MODEL, D_FF), f32) * D_MODEL ** -0.5,
        'w_ff2': jax.random.normal(ks[13], (DEPTH, D_FF, D_MODEL), f32) * D_FF ** -0.5,
        'g_final': gain(ks[14], (D_MODEL,)),
    }


def reference(x_prompt, x_sample, mem_prompt, mem_sample, g_mix, w_in, g_mem, w_mem_kv, sinks, g_grp, w_out, g_ffn, w_ff1, w_ff2, g_final):
    y_prompt = trunk(x_prompt, mem_prompt, g_mix, w_in, g_mem, w_mem_kv, sinks, g_grp, w_out, g_ffn, w_ff1, w_ff2, g_final)
    y_sample = trunk(x_sample, mem_sample, g_mix, w_in, g_mem, w_mem_kv, sinks, g_grp, w_out, g_ffn, w_ff1, w_ff2, g_final)
    return (y_prompt, y_sample)
```

```python
import functools
import math

import jax
import jax.numpy as jnp
from jax import lax
from jax.experimental import pallas as pl
from jax.experimental.pallas import tpu as pltpu

F32 = jnp.float32
BF16 = jnp.bfloat16

D_MODEL = 1024
HEAD_DIM = 64
N_HEADS = 8
N_KV_HEADS = 2
GQA_GROUP = 4
WINDOW = 128
BLOCK = 128
N_MEM = 256
N_MEM_HEADS = 4
ATTN_WIDTH = 512
KV_WIDTH = 128
FOURIER_WIDTH = 256
FOURIER_GROUP_WIDTH = 64
MEM_WIDTH = 256
IN_WIDTH = 1280
D_FF = 4096
EPS = 1e-6
NEG_INF = -1e30

LANES = 128
VMEM_LIMIT = 56 * 1024 * 1024

ROW_TILE = 512
FF_CHUNK = 1024


def _rms(x, g):
    ms = jnp.mean(x * x, axis=-1, keepdims=True)
    return x * lax.rsqrt(ms + EPS) * g


def _inproj_kernel(x_ref, g_ref, w_ref, q_ref, k_ref, v_ref, u_ref, qm_ref):
    h = _rms(x_ref[...], g_ref[...]).astype(BF16)
    p = jnp.dot(h, w_ref[...], preferred_element_type=F32)
    scale = HEAD_DIM ** -0.5
    o1 = ATTN_WIDTH
    o2 = o1 + KV_WIDTH
    o3 = o2 + KV_WIDTH
    o4 = o3 + FOURIER_WIDTH
    q_ref[...] = (p[:, :o1] * scale).astype(BF16)
    k_ref[...] = p[:, o1:o2].astype(BF16)
    v_ref[...] = p[:, o2:o3].astype(BF16)
    u_ref[...] = p[:, o3:o4]
    qm_ref[...] = (p[:, o4:] * scale).astype(BF16)


def _inproj(x2, g, w):
    t = x2.shape[0]
    tm = ROW_TILE
    row = lambda i: (i, 0)
    fixed = lambda i: (0, 0)
    return pl.pallas_call(
        _inproj_kernel,
        grid=(t // tm,),
        in_specs=[
            pl.BlockSpec((tm, D_MODEL), row),
            pl.BlockSpec((1, D_MODEL), fixed),
            pl.BlockSpec((D_MODEL, IN_WIDTH), fixed),
        ],
        out_specs=[
            pl.BlockSpec((tm, ATTN_WIDTH), row),
            pl.BlockSpec((tm, KV_WIDTH), row),
            pl.BlockSpec((tm, KV_WIDTH), row),
            pl.BlockSpec((tm, FOURIER_WIDTH), row),
            pl.BlockSpec((tm, MEM_WIDTH), row),
        ],
        out_shape=[
            jax.ShapeDtypeStruct((t, ATTN_WIDTH), BF16),
            jax.ShapeDtypeStruct((t, KV_WIDTH), BF16),
            jax.ShapeDtypeStruct((t, KV_WIDTH), BF16),
            jax.ShapeDtypeStruct((t, FOURIER_WIDTH), F32),
            jax.ShapeDtypeStruct((t, MEM_WIDTH), BF16),
        ],
        compiler_params=pltpu.CompilerParams(
            dimension_semantics=("arbitrary",), vmem_limit_bytes=VMEM_LIMIT),
        name="inproj",
    )(x2, g, w)


def _cmul_const(re, im, c, s):
    def near(a, b):
        return abs(a - b) < 1e-12
    if near(c, 1.0) and near(s, 0.0):
        return re, im
    if near(c, -1.0) and near(s, 0.0):
        return -re, -im
    if near(c, 0.0) and near(s, -1.0):
        return im, -re
    if near(c, 0.0) and near(s, 1.0):
        return -im, re
    return re * c - im * s, re * s + im * c


def _small_dft(xs):
    n = len(xs)
    if n == 1:
        return xs
    ev = _small_dft(xs[0::2])
    od = _small_dft(xs[1::2])
    out = [None] * n
    for k in range(n // 2):
        ang = -2.0 * math.pi * k / n
        tr, ti = _cmul_const(od[k][0], od[k][1], math.cos(ang), math.sin(ang))
        out[k] = (ev[k][0] + tr, ev[k][1] + ti)
        out[k + n // 2] = (ev[k][0] - tr, ev[k][1] - ti)
    return out


def _fourier_kernel(u_ref, f_ref, twr_ref, twi_ref, cm_ref, o_ref, zr_ref, zi_ref,
                    *, n1, n2, m_chunk, r_chunk, scale):
    def stage1(p, carry):
        xa = u_ref[pl.ds(2 * p, n1, stride=n2), :]
        xb = u_ref[pl.ds(2 * p + 1, n1, stride=n2), :]
        x2 = jnp.concatenate([xa, xb], axis=1).astype(BF16)
        for mc in range(n1 // m_chunk):
            rows = pl.ds(mc * m_chunk, m_chunk)
            zr = jnp.dot(f_ref[pl.ds(mc * m_chunk, m_chunk), :], x2, preferred_element_type=F32)
            zi = jnp.dot(f_ref[pl.ds(n1 + mc * m_chunk, m_chunk), :], x2, preferred_element_type=F32)
            zr_ref[2 * p, rows, :] = zr[:, :LANES]
            zr_ref[2 * p + 1, rows, :] = zr[:, LANES:]
            zi_ref[2 * p, rows, :] = zi[:, :LANES]
            zi_ref[2 * p + 1, rows, :] = zi[:, LANES:]
        return carry
    lax.fori_loop(0, n2 // 2, stage1, 0)

    def stage2(c, carry):
        r0 = pl.multiple_of(c * r_chunk, r_chunk)
        rows = pl.ds(r0, r_chunk)
        twr = twr_ref[rows, :]
        twi = twi_ref[rows, :]
        xs = []
        for j in range(n2):
            re = zr_ref[j, rows, :]
            im = zi_ref[j, rows, :]
            if j > 0:
                cr = twr[:, j:j + 1]
                ci = twi[:, j:j + 1]
                re, im = re * cr - im * ci, re * ci + im * cr
            xs.append((re, im))
        ys = _small_dft(xs)
        for j in range(n2):
            zr_ref[j, rows, :] = ys[j][0]
            zi_ref[j, rows, :] = ys[j][1]
        return carry
    lax.fori_loop(0, n1 // r_chunk, stage2, 0)

    def stage3(j, carry):
        for mc in range(n1 // m_chunk):
            rows = pl.ds(mc * m_chunk, m_chunk)
            xc = jnp.concatenate([zr_ref[j, rows, :], zi_ref[j, rows, :]], axis=1).astype(BF16)
            y = jnp.dot(xc, cm_ref[...], preferred_element_type=F32)
            o0 = pl.multiple_of(j * n1 + mc * m_chunk, m_chunk)
            o_ref[pl.ds(o0, m_chunk), :] = y * scale
        return carry
    lax.fori_loop(0, n2, stage3, 0)


def _fourier_plan(s):
    n2 = 4 if s <= 4096 else 16
    return s // n2, n2


def _fourier_consts(s):
    n1, n2 = _fourier_plan(s)
    k = jnp.arange(n1, dtype=jnp.int32)
    ang = ((k[:, None] * k[None, :]) % n1).astype(F32) * (2.0 * math.pi / n1)
    fmat = jnp.concatenate([jnp.cos(ang), -jnp.sin(ang)], axis=0).astype(BF16)
    j = jnp.arange(LANES, dtype=jnp.int32)
    tang = ((k[:, None] * j[None, :]) % s).astype(F32) * (2.0 * math.pi / s)
    live = j[None, :] < n2
    twr = jnp.where(live, jnp.cos(tang), 0.0)
    twi = jnp.where(live, -jnp.sin(tang), 0.0)
    c = jnp.arange(LANES, dtype=jnp.int32)
    same = (c[:, None] // FOURIER_GROUP_WIDTH) == (c[None, :] // FOURIER_GROUP_WIDTH)
    cang = ((c[:, None] * c[None, :]) % FOURIER_GROUP_WIDTH).astype(F32) * (2.0 * math.pi / FOURIER_GROUP_WIDTH)
    cmat = jnp.concatenate([jnp.where(same, jnp.cos(cang), 0.0),
                            jnp.where(same, jnp.sin(cang), 0.0)], axis=0).astype(BF16)
    return fmat, twr, twi, cmat


def _fourier(u, consts):
    b, s, _ = u.shape
    n1, n2 = _fourier_plan(s)
    fmat, twr, twi, cmat = consts
    kern = functools.partial(
        _fourier_kernel, n1=n1, n2=n2, m_chunk=min(n1, 512), r_chunk=8,
        scale=float((s * FOURIER_GROUP_WIDTH) ** -0.5))
    fixed = lambda i, j: (0, 0)
    once = pl.Buffered(1)
    return pl.pallas_call(
        kern,
        grid=(b, FOURIER_WIDTH // LANES),
        in_specs=[
            pl.BlockSpec((None, s, LANES), lambda i, j: (i, 0, j), pipeline_mode=once),
            pl.BlockSpec((2 * n1, n1), fixed, pipeline_mode=once),
            pl.BlockSpec((n1, LANES), fixed, pipeline_mode=once),
            pl.BlockSpec((n1, LANES), fixed, pipeline_mode=once),
            pl.BlockSpec((2 * LANES, LANES), fixed, pipeline_mode=once),
        ],
        out_specs=pl.BlockSpec((None, s, LANES), lambda i, j: (i, 0, j), pipeline_mode=once),
        out_shape=jax.ShapeDtypeStruct((b, s, FOURIER_WIDTH), F32),
        scratch_shapes=[pltpu.VMEM((n2, n1, LANES), F32), pltpu.VMEM((n2, n1, LANES), F32)],
        compiler_params=pltpu.CompilerParams(
            dimension_semantics=("arbitrary", "arbitrary"), vmem_limit_bytes=VMEM_LIMIT),
        name="fourier",
    )(u, fmat, twr, twi, cmat)


def _memkv_kernel(m_ref, g_ref, w_ref, kbt_ref, vbd_ref):
    h = _rms(m_ref[...], g_ref[...]).astype(BF16)
    kv = jnp.dot(h, w_ref[...], preferred_element_type=F32)
    kt = kv[:, :MEM_WIDTH].T
    vm = kv[:, MEM_WIDTH:]
    ch_row = lax.broadcasted_iota(jnp.int32, (MEM_WIDTH, N_MEM), 0) // HEAD_DIM
    ch_col = lax.broadcasted_iota(jnp.int32, (N_MEM, MEM_WIDTH), 1) // HEAD_DIM
    for hd in range(N_MEM_HEADS):
        kbt_ref[:, hd * N_MEM:(hd + 1) * N_MEM] = jnp.where(ch_row == hd, kt, 0.0).astype(BF16)
        vbd_ref[hd * N_MEM:(hd + 1) * N_MEM, :] = jnp.where(ch_col == hd, vm, 0.0).astype(BF16)


def _memkv(mem, g, w):
    b = mem.shape[0]
    fixed = lambda i: (0, 0)
    return pl.pallas_call(
        _memkv_kernel,
        grid=(b,),
        in_specs=[
            pl.BlockSpec((None, N_MEM, D_MODEL), lambda i: (i, 0, 0)),
            pl.BlockSpec((1, D_MODEL), fixed),
            pl.BlockSpec((D_MODEL, 2 * MEM_WIDTH), fixed),
        ],
        out_specs=[
            pl.BlockSpec((None, MEM_WIDTH, N_MEM_HEADS * N_MEM), lambda i: (i, 0, 0)),
            pl.BlockSpec((None, N_MEM_HEADS * N_MEM, MEM_WIDTH), lambda i: (i, 0, 0)),
        ],
        out_shape=[
            jax.ShapeDtypeStruct((b, MEM_WIDTH, N_MEM_HEADS * N_MEM), BF16),
            jax.ShapeDtypeStruct((b, N_MEM_HEADS * N_MEM, MEM_WIDTH), BF16),
        ],
        compiler_params=pltpu.CompilerParams(
            dimension_semantics=("arbitrary",), vmem_limit_bytes=VMEM_LIMIT),
        name="memkv",
    )(mem, g, w)


def _mixer_kernel(x_ref, q_ref, kp_ref, kc_ref, kn_ref, vp_ref, vc_ref, vn_ref, qm_ref, yf_ref,
                  kbt_ref, vbd_ref, sink_ref, ggrp_ref, wout_ref, gffn_ref, w1_ref, w2_ref, gfin_ref,
                  o_ref, kprep_ref, vprep_ref, ymix_ref, x1_ref,
                  *, tq, seq_len, final):
    nqb = tq // BLOCK
    nkb = nqb + 2
    jblock0 = pl.program_id(1) * nqb

    lane = lax.broadcasted_iota(jnp.int32, (BLOCK, LANES), 1)
    low = lane < HEAD_DIM

    def prep(dst_ref, blk, t):
        tf = t.astype(F32)
        tr = pltpu.roll(tf, HEAD_DIM, axis=1)
        zero = jnp.zeros_like(tf)
        dst_ref[blk, 0, 0] = jnp.where(low, tf, zero).astype(BF16)
        dst_ref[blk, 0, 1] = jnp.where(low, zero, tr).astype(BF16)
        dst_ref[blk, 1, 0] = jnp.where(low, tr, zero).astype(BF16)
        dst_ref[blk, 1, 1] = jnp.where(low, zero, tf).astype(BF16)

    for src_k, src_v, blk, sl in (
            [(kp_ref, vp_ref, 0, slice(None))]
            + [(kc_ref, vc_ref, 1 + i, pl.ds(i * BLOCK, BLOCK)) for i in range(nqb)]
            + [(kn_ref, vn_ref, nkb - 1, slice(None))]):
        prep(kprep_ref, blk, src_k[sl, :])
        prep(vprep_ref, blk, src_v[sl, :])

    qa = lax.broadcasted_iota(jnp.int32, (BLOCK, 3 * BLOCK), 0)
    kc = lax.broadcasted_iota(jnp.int32, (BLOCK, 3 * BLOCK), 1)
    rel = kc - BLOCK - qa
    dist = jnp.abs(rel).astype(F32)
    in_window = jnp.abs(rel) <= WINDOW
    lane_q = lax.broadcasted_iota(jnp.int32, (BLOCK, LANES), 1)
    lane_m = lax.broadcasted_iota(jnp.int32, (BLOCK, MEM_WIDTH), 1) // HEAD_DIM
    nt_dims = (((1,), (1,)), ((), ()))

    def block_body(qb, carry):
        r0 = pl.multiple_of(qb * BLOCK, BLOCK)
        rows = pl.ds(r0, BLOCK)
        key_pos = (jblock0 + qb - 1) * BLOCK + kc
        valid = in_window & (key_pos >= 0) & (key_pos < seq_len)
        valid2 = jnp.concatenate([valid, valid], axis=1)
        dist2 = jnp.concatenate([dist, dist], axis=1)
        col2 = lax.broadcasted_iota(jnp.int32, (BLOCK, 6 * BLOCK), 1)

        outs = []
        for kh in range(N_KV_HEADS):
            kmat = jnp.concatenate(
                [kprep_ref[qb + j, kh, ab] for ab in range(2) for j in range(3)], axis=0)
            vmat = jnp.concatenate(
                [vprep_ref[qb + j, kh, ab] for ab in range(2) for j in range(3)], axis=0)
            for hf in range(2):
                h0 = kh * GQA_GROUP + 2 * hf
                c0 = kh * GQA_GROUP * HEAD_DIM + hf * LANES
                qh = q_ref[rows, c0:c0 + LANES]
                s = lax.dot_general(qh, kmat, nt_dims, preferred_element_type=F32)
                slope2 = jnp.where(col2 < 3 * BLOCK, -(2.0 ** -(h0 + 1)), -(2.0 ** -(h0 + 2)))
                s = jnp.where(valid2, s + slope2 * dist2, NEG_INF)
                ps, inv = [], []
                for e in range(2):
                    se = s[:, e * 3 * BLOCK:(e + 1) * 3 * BLOCK]
                    sink = sink_ref[h0 + e]
                    m = jnp.maximum(jnp.max(se, axis=-1, keepdims=True), sink)
                    p = jnp.exp(se - m)
                    l = jnp.sum(p, axis=-1, keepdims=True) + jnp.exp(sink - m)
                    ps.append(p.astype(BF16))
                    inv.append(1.0 / l)
                pv = jnp.dot(jnp.concatenate(ps, axis=1), vmat, preferred_element_type=F32)
                outs.append(pv * jnp.where(lane_q < HEAD_DIM, inv[0], inv[1]))
        y_attn = jnp.concatenate(outs, axis=1)

        sm = jnp.dot(qm_ref[rows, :], kbt_ref[...], preferred_element_type=F32)
        pm, invm = [], []
        for hd in range(N_MEM_HEADS):
            se = sm[:, hd * N_MEM:(hd + 1) * N_MEM]
            m = jnp.max(se, axis=-1, keepdims=True)
            p = jnp.exp(se - m)
            invm.append(1.0 / jnp.sum(p, axis=-1, keepdims=True))
            pm.append(p.astype(BF16))
        om = jnp.dot(jnp.concatenate(pm, axis=1), vbd_ref[...], preferred_element_type=F32)
        scale_m = jnp.where(lane_m == 0, invm[0],
                            jnp.where(lane_m == 1, invm[1], jnp.where(lane_m == 2, invm[2], invm[3])))
        y_mem = om * scale_m

        g = ggrp_ref[...]
        o1 = ATTN_WIDTH
        o2 = o1 + FOURIER_WIDTH
        ymix_ref[rows, :o1] = _rms(y_attn, g[:, :o1]).astype(BF16)
        ymix_ref[rows, o1:o2] = _rms(yf_ref[rows, :], g[:, o1:o2]).astype(BF16)
        ymix_ref[rows, o2:] = _rms(y_mem, g[:, o2:]).astype(BF16)
        return carry

    lax.fori_loop(0, nqb, block_body, 0)

    x1 = x_ref[...] + jnp.dot(ymix_ref[...], wout_ref[...], preferred_element_type=F32)
    x1_ref[...] = x1
    hn = _rms(x1, gffn_ref[...]).astype(BF16)
    acc = None
    for c in range(D_FF // FF_CHUNK):
        cols = slice(c * FF_CHUNK, (c + 1) * FF_CHUNK)
        hf = jnp.dot(hn, w1_ref[:, cols], preferred_element_type=F32)
        hf = jnp.square(jnp.maximum(hf, 0.0)).astype(BF16)
        part = jnp.dot(hf, w2_ref[cols, :], preferred_element_type=F32)
        acc = part if acc is None else acc + part
    y = x1_ref[...] + acc
    if final:
        y = _rms(y, gfin_ref[...])
    o_ref[...] = y


def _mixer(x, q, k, v, qm, yf, kbt, vbd, sinks, g_grp, w_out, g_ffn, w1, w2, g_final, final):
    b, s, _ = x.shape
    tq = ROW_TILE
    nqb = tq // BLOCK
    last_blk = s // BLOCK - 1
    tile = lambda i, j: (i, j, 0)
    prev = lambda i, j: (i, jnp.maximum(j * nqb - 1, 0), 0)
    nxt = lambda i, j: (i, jnp.minimum((j + 1) * nqb, last_blk), 0)
    per_b = lambda i, j: (i, 0, 0)
    fixed = lambda i, j: (0, 0)
    once = pl.Buffered(1)
    kern = functools.partial(_mixer_kernel, tq=tq, seq_len=s, final=final)
    return pl.pallas_call(
        kern,
        grid=(b, s // tq),
        in_specs=[
            pl.BlockSpec((None, tq, D_MODEL), tile),
            pl.BlockSpec((None, tq, ATTN_WIDTH), tile),
            pl.BlockSpec((None, BLOCK, KV_WIDTH), prev),
            pl.BlockSpec((None, tq, KV_WIDTH), tile),
            pl.BlockSpec((None, BLOCK, KV_WIDTH), nxt),
            pl.BlockSpec((None, BLOCK, KV_WIDTH), prev),
            pl.BlockSpec((None, tq, KV_WIDTH), tile),
            pl.BlockSpec((None, BLOCK, KV_WIDTH), nxt),
            pl.BlockSpec((None, tq, MEM_WIDTH), tile),
            pl.BlockSpec((None, tq, FOURIER_WIDTH), tile),
            pl.BlockSpec((None, MEM_WIDTH, N_MEM_HEADS * N_MEM), per_b),
            pl.BlockSpec((None, N_MEM_HEADS * N_MEM, MEM_WIDTH), per_b),
            pl.BlockSpec(memory_space=pltpu.SMEM),
            pl.BlockSpec((1, D_MODEL), fixed),
            pl.BlockSpec((D_MODEL, D_MODEL), fixed, pipeline_mode=once),
            pl.BlockSpec((1, D_MODEL), fixed),
            pl.BlockSpec((D_MODEL, D_FF), fixed, pipeline_mode=once),
            pl.BlockSpec((D_FF, D_MODEL), fixed, pipeline_mode=once),
            pl.BlockSpec((1, D_MODEL), fixed),
        ],
        out_specs=pl.BlockSpec((None, tq, D_MODEL), tile),
        out_shape=jax.ShapeDtypeStruct((b, s, D_MODEL), F32),
        scratch_shapes=[
            pltpu.VMEM((nqb + 2, N_KV_HEADS, 2, BLOCK, LANES), BF16),
            pltpu.VMEM((nqb + 2, N_KV_HEADS, 2, BLOCK, LANES), BF16),
            pltpu.VMEM((tq, D_MODEL), BF16),
            pltpu.VMEM((tq, D_MODEL), F32),
        ],
        compiler_params=pltpu.CompilerParams(
            dimension_semantics=("arbitrary", "arbitrary"), vmem_limit_bytes=VMEM_LIMIT),
        name="mixer",
    )(x, q, k, k, k, v, v, v, qm, yf, kbt, vbd, sinks, g_grp, w_out, g_ffn, w1, w2, g_final)


def _trunk(x, mem, g_mix, w_in, g_mem, w_mem_kv, sinks, g_grp, w_out, g_ffn, w_ff1, w_ff2, g_final):
    b, s, d = x.shape
    depth = g_mix.shape[0]
    consts = _fourier_consts(s)
    row = lambda a: a.reshape(1, -1)
    for l in range(depth):
        q, k, v, u, qm = _inproj(x.reshape(b * s, d), row(g_mix[l]), w_in[l])
        shp = lambda a: a.reshape(b, s, a.shape[-1])
        yf = _fourier(shp(u), consts)
        kbt, vbd = _memkv(mem, row(g_mem[l]), w_mem_kv[l])
        x = _mixer(x, shp(q), shp(k), shp(v), shp(qm), yf, kbt, vbd, sinks[l], row(g_grp[l]), w_out[l],
                   row(g_ffn[l]), w_ff1[l], w_ff2[l], row(g_final), final=(l == depth - 1))
    return x


def kernel(x_prompt, x_sample, mem_prompt, mem_sample, g_mix, w_in, g_mem, w_mem_kv, sinks, g_grp, w_out,
           g_ffn, w_ff1, w_ff2, g_final):
    weights = (g_mix, w_in.astype(BF16), g_mem, w_mem_kv.astype(BF16), sinks, g_grp, w_out.astype(BF16),
               g_ffn, w_ff1.astype(BF16), w_ff2.astype(BF16), g_final)
    y_prompt = _trunk(x_prompt, mem_prompt, *weights)
    y_sample = _trunk(x_sample, mem_sample, *weights)
    return (y_prompt, y_sample)
```

```python
import functools
import math

import jax
import jax.numpy as jnp
from jax import lax
from jax.experimental import pallas as pl
from jax.experimental.pallas import tpu as pltpu

F32 = jnp.float32
BF16 = jnp.bfloat16

D_MODEL = 1024
HEAD_DIM = 64
N_HEADS = 8
N_KV_HEADS = 2
GQA_GROUP = 4
WINDOW = 128
BLOCK = 128
N_MEM = 256
N_MEM_HEADS = 4
ATTN_WIDTH = 512
KV_WIDTH = 128
FOURIER_WIDTH = 256
FOURIER_GROUP_WIDTH = 64
MEM_WIDTH = 256
IN_WIDTH = 1280
D_FF = 4096
EPS = 1e-6
NEG_INF = -1e30
LOG2E = 1.4426950408889634

LANES = 128
VMEM_LIMIT = 56 * 1024 * 1024

ROW_TILE = 512
FF_CHUNK = 1024


def _rms(x, g):
    ms = jnp.mean(x * x, axis=-1, keepdims=True)
    return x * lax.rsqrt(ms + EPS) * g


def _inproj_kernel(x_ref, g_ref, w_ref, q_ref, k_ref, v_ref, u_ref, qm_ref):
    h = _rms(x_ref[...], g_ref[...]).astype(BF16)
    p = jnp.dot(h, w_ref[...], preferred_element_type=F32)
    scale = LOG2E * HEAD_DIM ** -0.5
    o1 = ATTN_WIDTH
    o2 = o1 + KV_WIDTH
    o3 = o2 + KV_WIDTH
    o4 = o3 + FOURIER_WIDTH
    q_ref[...] = (p[:, :o1] * scale).astype(BF16)
    k_ref[...] = p[:, o1:o2].astype(BF16)
    v_ref[...] = p[:, o2:o3].astype(BF16)
    u_ref[...] = p[:, o3:o4]
    qm_ref[...] = (p[:, o4:] * scale).astype(BF16)


def _inproj(x2, g, w):
    t = x2.shape[0]
    tm = ROW_TILE
    row = lambda i: (i, 0)
    fixed = lambda i: (0, 0)
    return pl.pallas_call(
        _inproj_kernel,
        grid=(t // tm,),
        in_specs=[
            pl.BlockSpec((tm, D_MODEL), row),
            pl.BlockSpec((1, D_MODEL), fixed),
            pl.BlockSpec((D_MODEL, IN_WIDTH), fixed),
        ],
        out_specs=[
            pl.BlockSpec((tm, ATTN_WIDTH), row),
            pl.BlockSpec((tm, KV_WIDTH), row),
            pl.BlockSpec((tm, KV_WIDTH), row),
            pl.BlockSpec((tm, FOURIER_WIDTH), row),
            pl.BlockSpec((tm, MEM_WIDTH), row),
        ],
        out_shape=[
            jax.ShapeDtypeStruct((t, ATTN_WIDTH), BF16),
            jax.ShapeDtypeStruct((t, KV_WIDTH), BF16),
            jax.ShapeDtypeStruct((t, KV_WIDTH), BF16),
            jax.ShapeDtypeStruct((t, FOURIER_WIDTH), F32),
            jax.ShapeDtypeStruct((t, MEM_WIDTH), BF16),
        ],
        compiler_params=pltpu.CompilerParams(
            dimension_semantics=("arbitrary",), vmem_limit_bytes=VMEM_LIMIT),
        name="inproj",
    )(x2, g, w)


def _cmul_const(re, im, c, s):
    def near(a, b):
        return abs(a - b) < 1e-12
    if near(c, 1.0) and near(s, 0.0):
        return re, im
    if near(c, -1.0) and near(s, 0.0):
        return -re, -im
    if near(c, 0.0) and near(s, -1.0):
        return im, -re
    if near(c, 0.0) and near(s, 1.0):
        return -im, re
    return re * c - im * s, re * s + im * c


def _small_dft(xs):
    n = len(xs)
    if n == 1:
        return xs
    ev = _small_dft(xs[0::2])
    od = _small_dft(xs[1::2])
    out = [None] * n
    for k in range(n // 2):
        ang = -2.0 * math.pi * k / n
        tr, ti = _cmul_const(od[k][0], od[k][1], math.cos(ang), math.sin(ang))
        out[k] = (ev[k][0] + tr, ev[k][1] + ti)
        out[k + n // 2] = (ev[k][0] - tr, ev[k][1] - ti)
    return out


def _fourier_kernel(u_ref, f_ref, twr_ref, twi_ref, cm_ref, o_ref, zr_ref, zi_ref,
                    *, n1, n2, m_chunk, r_chunk, scale, wide_twiddles, unroll):
    def stage1(p, carry):
        xa = u_ref[pl.ds(2 * p, n1, stride=n2), :]
        xb = u_ref[pl.ds(2 * p + 1, n1, stride=n2), :]
        x2 = jnp.concatenate([xa, xb], axis=1).astype(BF16)
        for mc in range(n1 // m_chunk):
            rows = pl.ds(mc * m_chunk, m_chunk)
            zr = jnp.dot(f_ref[pl.ds(mc * m_chunk, m_chunk), :], x2, preferred_element_type=F32)
            zi = jnp.dot(f_ref[pl.ds(n1 + mc * m_chunk, m_chunk), :], x2, preferred_element_type=F32)
            zr_ref[2 * p, rows, :] = zr[:, :LANES]
            zr_ref[2 * p + 1, rows, :] = zr[:, LANES:]
            zi_ref[2 * p, rows, :] = zi[:, :LANES]
            zi_ref[2 * p + 1, rows, :] = zi[:, LANES:]
        return carry
    lax.fori_loop(0, n2 // 2, stage1, 0, unroll=unroll)

    def stage2(c, carry):
        r0 = pl.multiple_of(c * r_chunk, r_chunk)
        rows = pl.ds(r0, r_chunk)
        if not wide_twiddles:
            twr = twr_ref[rows, :]
            twi = twi_ref[rows, :]
        xs = []
        for j in range(n2):
            re = zr_ref[j, rows, :]
            im = zi_ref[j, rows, :]
            if j > 0:
                if wide_twiddles:
                    cr = twr_ref[j, rows, :]
                    ci = twi_ref[j, rows, :]
                else:
                    cr = twr[:, j:j + 1]
                    ci = twi[:, j:j + 1]
                re, im = re * cr - im * ci, re * ci + im * cr
            xs.append((re, im))
        ys = _small_dft(xs)
        for j in range(n2):
            zr_ref[j, rows, :] = ys[j][0]
            zi_ref[j, rows, :] = ys[j][1]
        return carry
    lax.fori_loop(0, n1 // r_chunk, stage2, 0)

    def stage3(j, carry):
        for mc in range(n1 // m_chunk):
            rows = pl.ds(mc * m_chunk, m_chunk)
            xc = jnp.concatenate([zr_ref[j, rows, :], zi_ref[j, rows, :]], axis=1).astype(BF16)
            y = jnp.dot(xc, cm_ref[...], preferred_element_type=F32)
            o0 = pl.multiple_of(j * n1 + mc * m_chunk, m_chunk)
            o_ref[pl.ds(o0, m_chunk), :] = y * scale
        return carry
    lax.fori_loop(0, n2, stage3, 0, unroll=True if unroll else 2)


def _fourier_plan(s):
    n2 = 4 if s <= 4096 else 16
    return s // n2, n2


def _wide_twiddles(n2):
    return n2 <= 4


def _fourier_consts(s):
    n1, n2 = _fourier_plan(s)
    k = jnp.arange(n1, dtype=jnp.int32)
    ang = ((k[:, None] * k[None, :]) % n1).astype(F32) * (2.0 * math.pi / n1)
    fmat = jnp.concatenate([jnp.cos(ang), -jnp.sin(ang)], axis=0).astype(BF16)
    if _wide_twiddles(n2):
        j = jnp.arange(n2, dtype=jnp.int32)
        tang = ((j[:, None] * k[None, :]) % s).astype(F32) * (2.0 * math.pi / s)
        twr = jnp.broadcast_to(jnp.cos(tang)[:, :, None], (n2, n1, LANES))
        twi = jnp.broadcast_to(-jnp.sin(tang)[:, :, None], (n2, n1, LANES))
    else:
        j = jnp.arange(LANES, dtype=jnp.int32)
        tang = ((k[:, None] * j[None, :]) % s).astype(F32) * (2.0 * math.pi / s)
        live = j[None, :] < n2
        twr = jnp.where(live, jnp.cos(tang), 0.0)
        twi = jnp.where(live, -jnp.sin(tang), 0.0)
    c = jnp.arange(LANES, dtype=jnp.int32)
    same = (c[:, None] // FOURIER_GROUP_WIDTH) == (c[None, :] // FOURIER_GROUP_WIDTH)
    cang = ((c[:, None] * c[None, :]) % FOURIER_GROUP_WIDTH).astype(F32) * (2.0 * math.pi / FOURIER_GROUP_WIDTH)
    cmat = jnp.concatenate([jnp.where(same, jnp.cos(cang), 0.0),
                            jnp.where(same, jnp.sin(cang), 0.0)], axis=0).astype(BF16)
    return fmat, twr, twi, cmat


def _fourier(u, consts):
    b, s, _ = u.shape
    n1, n2 = _fourier_plan(s)
    fmat, twr, twi, cmat = consts
    wide = _wide_twiddles(n2)
    kern = functools.partial(
        _fourier_kernel, n1=n1, n2=n2, m_chunk=min(n1, 512), r_chunk=32,
        scale=float((s * FOURIER_GROUP_WIDTH) ** -0.5), wide_twiddles=wide, unroll=wide)
    fixed = lambda i, j: (0, 0)
    fixed3 = lambda i, j: (0, 0, 0)
    once = pl.Buffered(1)
    seq_mode = once if s * LANES * 4 > 4 * 1024 * 1024 else pl.Buffered(2)
    tw_spec = (pl.BlockSpec((n2, n1, LANES), fixed3, pipeline_mode=once) if wide
               else pl.BlockSpec((n1, LANES), fixed, pipeline_mode=once))
    return pl.pallas_call(
        kern,
        grid=(b, FOURIER_WIDTH // LANES),
        in_specs=[
            pl.BlockSpec((None, s, LANES), lambda i, j: (i, 0, j), pipeline_mode=seq_mode),
            pl.BlockSpec((2 * n1, n1), fixed, pipeline_mode=once),
            tw_spec,
            tw_spec,
            pl.BlockSpec((2 * LANES, LANES), fixed, pipeline_mode=once),
        ],
        out_specs=pl.BlockSpec((None, s, LANES), lambda i, j: (i, 0, j), pipeline_mode=seq_mode),
        out_shape=jax.ShapeDtypeStruct((b, s, FOURIER_WIDTH), F32),
        scratch_shapes=[pltpu.VMEM((n2, n1, LANES), F32), pltpu.VMEM((n2, n1, LANES), F32)],
        compiler_params=pltpu.CompilerParams(
            dimension_semantics=("arbitrary", "arbitrary"), vmem_limit_bytes=VMEM_LIMIT),
        name="fourier",
    )(u, fmat, twr, twi, cmat)


def _memkv_kernel(m_ref, g_ref, w_ref, kbt_ref, vbd_ref):
    h = _rms(m_ref[...], g_ref[...]).astype(BF16)
    kv = jnp.dot(h, w_ref[...], preferred_element_type=F32)
    kt = kv[:, :MEM_WIDTH].T
    vm = kv[:, MEM_WIDTH:]
    ch_row = lax.broadcasted_iota(jnp.int32, (MEM_WIDTH, N_MEM), 0) // HEAD_DIM
    ch_col = lax.broadcasted_iota(jnp.int32, (N_MEM, MEM_WIDTH), 1) // HEAD_DIM
    for hd in range(N_MEM_HEADS):
        kbt_ref[:, hd * N_MEM:(hd + 1) * N_MEM] = jnp.where(ch_row == hd, kt, 0.0).astype(BF16)
        vbd_ref[hd * N_MEM:(hd + 1) * N_MEM, :] = jnp.where(ch_col == hd, vm, 0.0).astype(BF16)


def _memkv(mem, g, w):
    b = mem.shape[0]
    fixed = lambda i: (0, 0)
    return pl.pallas_call(
        _memkv_kernel,
        grid=(b,),
        in_specs=[
            pl.BlockSpec((None, N_MEM, D_MODEL), lambda i: (i, 0, 0)),
            pl.BlockSpec((1, D_MODEL), fixed),
            pl.BlockSpec((D_MODEL, 2 * MEM_WIDTH), fixed),
        ],
        out_specs=[
            pl.BlockSpec((None, MEM_WIDTH, N_MEM_HEADS * N_MEM), lambda i: (i, 0, 0)),
            pl.BlockSpec((None, N_MEM_HEADS * N_MEM, MEM_WIDTH), lambda i: (i, 0, 0)),
        ],
        out_shape=[
            jax.ShapeDtypeStruct((b, MEM_WIDTH, N_MEM_HEADS * N_MEM), BF16),
            jax.ShapeDtypeStruct((b, N_MEM_HEADS * N_MEM, MEM_WIDTH), BF16),
        ],
        compiler_params=pltpu.CompilerParams(
            dimension_semantics=("arbitrary",), vmem_limit_bytes=VMEM_LIMIT),
        name="memkv",
    )(mem, g, w)


def _mixer_kernel(x_ref, q_ref, kp_ref, kc_ref, kn_ref, vp_ref, vc_ref, vn_ref, qm_ref, yf_ref,
                  kbt_ref, vbd_ref, sink_ref, ggrp_ref, wout_ref, gffn_ref, w1_ref, w2_ref, gfin_ref,
                  o_ref, kprep_ref, vprep_ref, ymix_ref, x1_ref, bias_ref,
                  *, tq, seq_len, final):
    nqb = tq // BLOCK
    nkb = nqb + 2
    jblock0 = pl.program_id(1) * nqb

    lane = lax.broadcasted_iota(jnp.int32, (BLOCK, LANES), 1)
    low = lane < HEAD_DIM

    def prep(dst_ref, blk, t):
        tf = t.astype(F32)
        tr = pltpu.roll(tf, HEAD_DIM, axis=1)
        zero = jnp.zeros_like(tf)
        dst_ref[blk, 0, 0] = jnp.where(low, tf, zero).astype(BF16)
        dst_ref[blk, 0, 1] = jnp.where(low, zero, tr).astype(BF16)
        dst_ref[blk, 1, 0] = jnp.where(low, tr, zero).astype(BF16)
        dst_ref[blk, 1, 1] = jnp.where(low, zero, tf).astype(BF16)

    for src_k, src_v, blk, sl in (
            [(kp_ref, vp_ref, 0, slice(None))]
            + [(kc_ref, vc_ref, 1 + i, pl.ds(i * BLOCK, BLOCK)) for i in range(nqb)]
            + [(kn_ref, vn_ref, nkb - 1, slice(None))]):
        prep(kprep_ref, blk, src_k[sl, :])
        prep(vprep_ref, blk, src_v[sl, :])

    @pl.when((pl.program_id(0) == 0) & (pl.program_id(1) == 0))
    def _fill_bias():
        qa = lax.broadcasted_iota(jnp.int32, (BLOCK, 6 * BLOCK), 0)
        col = lax.broadcasted_iota(jnp.int32, (BLOCK, 6 * BLOCK), 1)
        second = col >= 3 * BLOCK
        kc = jnp.where(second, col - 3 * BLOCK, col)
        rel = kc - BLOCK - qa
        dist = jnp.abs(rel).astype(F32)
        in_window = jnp.abs(rel) <= WINDOW
        for variant, ok in enumerate((in_window, in_window & (kc >= BLOCK), in_window & (kc < 2 * BLOCK))):
            for pair in range(N_HEADS // 2):
                slope2 = jnp.where(second, -LOG2E * 2.0 ** -(2 * pair + 2), -LOG2E * 2.0 ** -(2 * pair + 1))
                bias_ref[variant, pair] = jnp.where(ok, slope2 * dist, NEG_INF)

    lane_q = lax.broadcasted_iota(jnp.int32, (BLOCK, LANES), 1)
    lane_m = lax.broadcasted_iota(jnp.int32, (BLOCK, MEM_WIDTH), 1) // HEAD_DIM
    nt_dims = (((1,), (1,)), ((), ()))
    last_block = seq_len // BLOCK - 1

    for qb in range(nqb):
        rows = pl.ds(qb * BLOCK, BLOCK)
        jb = jblock0 + qb
        variant = jnp.where(jb == 0, 1, jnp.where(jb == last_block, 2, 0))

        scores, vmats = [], []
        for kh in range(N_KV_HEADS):
            kmat = jnp.concatenate(
                [kprep_ref[qb + j, kh, ab] for ab in range(2) for j in range(3)], axis=0)
            vmats.append(jnp.concatenate(
                [vprep_ref[qb + j, kh, ab] for ab in range(2) for j in range(3)], axis=0))
            for hf in range(2):
                c0 = kh * GQA_GROUP * HEAD_DIM + hf * LANES
                qh = q_ref[rows, c0:c0 + LANES]
                s = lax.dot_general(qh, kmat, nt_dims, preferred_element_type=F32)
                scores.append(s + bias_ref[variant, kh * 2 + hf])
        sm = jnp.dot(qm_ref[rows, :], kbt_ref[...], preferred_element_type=F32)

        outs = []
        for pair, s in enumerate(scores):
            ps, inv = [], []
            for e in range(2):
                se = s[:, e * 3 * BLOCK:(e + 1) * 3 * BLOCK]
                sink = sink_ref[2 * pair + e] * LOG2E
                m = jnp.maximum(jnp.max(se, axis=-1, keepdims=True), sink)
                p = jnp.exp2(se - m)
                l = jnp.sum(p, axis=-1, keepdims=True) + jnp.exp2(sink - m)
                ps.append(p.astype(BF16))
                inv.append(1.0 / l)
            pv = jnp.dot(jnp.concatenate(ps, axis=1), vmats[pair // 2], preferred_element_type=F32)
            outs.append(pv * jnp.where(lane_q < HEAD_DIM, inv[0], inv[1]))
        y_attn = jnp.concatenate(outs, axis=1)

        pm, invm = [], []
        for hd in range(N_MEM_HEADS):
            se = sm[:, hd * N_MEM:(hd + 1) * N_MEM]
            m = jnp.max(se, axis=-1, keepdims=True)
            p = jnp.exp2(se - m)
            invm.append(1.0 / jnp.sum(p, axis=-1, keepdims=True))
            pm.append(p.astype(BF16))
        om = jnp.dot(jnp.concatenate(pm, axis=1), vbd_ref[...], preferred_element_type=F32)
        scale_m = jnp.where(lane_m == 0, invm[0],
                            jnp.where(lane_m == 1, invm[1], jnp.where(lane_m == 2, invm[2], invm[3])))
        y_mem = om * scale_m

        g = ggrp_ref[...]
        o1 = ATTN_WIDTH
        o2 = o1 + FOURIER_WIDTH
        ymix_ref[rows, :o1] = _rms(y_attn, g[:, :o1]).astype(BF16)
        ymix_ref[rows, o1:o2] = _rms(yf_ref[rows, :], g[:, o1:o2]).astype(BF16)
        ymix_ref[rows, o2:] = _rms(y_mem, g[:, o2:]).astype(BF16)

    x1 = x_ref[...] + jnp.dot(ymix_ref[...], wout_ref[...], preferred_element_type=F32)
    x1_ref[...] = x1
    hn = _rms(x1, gffn_ref[...]).astype(BF16)
    acc = None
    for c in range(D_FF // FF_CHUNK):
        cols = slice(c * FF_CHUNK, (c + 1) * FF_CHUNK)
        hf = jnp.dot(hn, w1_ref[:, cols], preferred_element_type=F32)
        hf = jnp.square(jnp.maximum(hf, 0.0)).astype(BF16)
        part = jnp.dot(hf, w2_ref[cols, :], preferred_element_type=F32)
        acc = part if acc is None else acc + part
    y = x1_ref[...] + acc
    if final:
        y = _rms(y, gfin_ref[...])
    o_ref[...] = y


def _mixer(x, q, k, v, qm, yf, kbt, vbd, sinks, g_grp, w_out, g_ffn, w1, w2, g_final, final):
    b, s, _ = x.shape
    tq = ROW_TILE
    nqb = tq // BLOCK
    last_blk = s // BLOCK - 1
    tile = lambda i, j: (i, j, 0)
    prev = lambda i, j: (i, jnp.maximum(j * nqb - 1, 0), 0)
    nxt = lambda i, j: (i, jnp.minimum((j + 1) * nqb, last_blk), 0)
    per_b = lambda i, j: (i, 0, 0)
    fixed = lambda i, j: (0, 0)
    once = pl.Buffered(1)
    kern = functools.partial(_mixer_kernel, tq=tq, seq_len=s, final=final)
    return pl.pallas_call(
        kern,
        grid=(b, s // tq),
        in_specs=[
            pl.BlockSpec((None, tq, D_MODEL), tile),
            pl.BlockSpec((None, tq, ATTN_WIDTH), tile),
            pl.BlockSpec((None, BLOCK, KV_WIDTH), prev),
            pl.BlockSpec((None, tq, KV_WIDTH), tile),
            pl.BlockSpec((None, BLOCK, KV_WIDTH), nxt),
            pl.BlockSpec((None, BLOCK, KV_WIDTH), prev),
            pl.BlockSpec((None, tq, KV_WIDTH), tile),
            pl.BlockSpec((None, BLOCK, KV_WIDTH), nxt),
            pl.BlockSpec((None, tq, MEM_WIDTH), tile),
            pl.BlockSpec((None, tq, FOURIER_WIDTH), tile),
            pl.BlockSpec((None, MEM_WIDTH, N_MEM_HEADS * N_MEM), per_b),
            pl.BlockSpec((None, N_MEM_HEADS * N_MEM, MEM_WIDTH), per_b),
            pl.BlockSpec(memory_space=pltpu.SMEM),
            pl.BlockSpec((1, D_MODEL), fixed),
            pl.BlockSpec((D_MODEL, D_MODEL), fixed, pipeline_mode=once),
            pl.BlockSpec((1, D_MODEL), fixed),
            pl.BlockSpec((D_MODEL, D_FF), fixed, pipeline_mode=once),
            pl.BlockSpec((D_FF, D_MODEL), fixed, pipeline_mode=once),
            pl.BlockSpec((1, D_MODEL), fixed),
        ],
        out_specs=pl.BlockSpec((None, tq, D_MODEL), tile),
        out_shape=jax.ShapeDtypeStruct((b, s, D_MODEL), F32),
        scratch_shapes=[
            pltpu.VMEM((nqb + 2, N_KV_HEADS, 2, BLOCK, LANES), BF16),
            pltpu.VMEM((nqb + 2, N_KV_HEADS, 2, BLOCK, LANES), BF16),
            pltpu.VMEM((tq, D_MODEL), BF16),
            pltpu.VMEM((tq, D_MODEL), F32),
            pltpu.VMEM((3, N_HEADS // 2, BLOCK, 6 * BLOCK), F32),
        ],
        compiler_params=pltpu.CompilerParams(
            dimension_semantics=("arbitrary", "arbitrary"), vmem_limit_bytes=VMEM_LIMIT),
        name="mixer",
    )(x, q, k, k, k, v, v, v, qm, yf, kbt, vbd, sinks, g_grp, w_out, g_ffn, w1, w2, g_final)


def _trunk(x, mem, g_mix, w_in, g_mem, w_mem_kv, sinks, g_grp, w_out, g_ffn, w_ff1, w_ff2, g_final):
    b, s, d = x.shape
    depth = g_mix.shape[0]
    consts = _fourier_consts(s)
    row = lambda a: a.reshape(1, -1)
    for l in range(depth):
        q, k, v, u, qm = _inproj(x.reshape(b * s, d), row(g_mix[l]), w_in[l])
        shp = lambda a: a.reshape(b, s, a.shape[-1])
        yf = _fourier(shp(u), consts)
        kbt, vbd = _memkv(mem, row(g_mem[l]), w_mem_kv[l])
        x = _mixer(x, shp(q), shp(k), shp(v), shp(qm), yf, kbt, vbd, sinks[l], row(g_grp[l]), w_out[l],
                   row(g_ffn[l]), w_ff1[l], w_ff2[l], row(g_final), final=(l == depth - 1))
    return x


def kernel(x_prompt, x_sample, mem_prompt, mem_sample, g_mix, w_in, g_mem, w_mem_kv, sinks, g_grp, w_out,
           g_ffn, w_ff1, w_ff2, g_final):
    weights = (g_mix, w_in.astype(BF16), g_mem, w_mem_kv.astype(BF16), sinks, g_grp, w_out.astype(BF16),
               g_ffn, w_ff1.astype(BF16), w_ff2.astype(BF16), g_final)
    y_prompt = _trunk(x_prompt, mem_prompt, *weights)
    y_sample = _trunk(x_sample, mem_sample, *weights)
    return (y_prompt, y_sample)
```

```python
import functools
import math

import jax
import jax.numpy as jnp
from jax import lax
from jax.experimental import pallas as pl
from jax.experimental.pallas import tpu as pltpu

F32 = jnp.float32
BF16 = jnp.bfloat16

D_MODEL = 1024
HEAD_DIM = 64
N_HEADS = 8
N_KV_HEADS = 2
GQA_GROUP = 4
WINDOW = 128
BLOCK = 128
N_MEM = 256
N_MEM_HEADS = 4
ATTN_WIDTH = 512
KV_WIDTH = 128
FOURIER_WIDTH = 256
FOURIER_GROUP_WIDTH = 64
MEM_WIDTH = 256
IN_WIDTH = 1280
D_FF = 4096
EPS = 1e-6
NEG_INF = -1e30
LOG2E = 1.4426950408889634

LANES = 128
VMEM_LIMIT = 56 * 1024 * 1024

LOOKAHEAD = 2
ROW_TILE = 512
INPROJ_ROWS = 1024
MEMKV_SEQS = 4
FF_CHUNK = 1024


def _rms(x, g):
    ms = jnp.mean(x * x, axis=-1, keepdims=True)
    return x * lax.rsqrt(ms + EPS) * g


def _alternate(gens, weights):
    live = [[g, w, 0.0] for g, w in zip(gens, weights)]
    while live:
        for item in list(live):
            item[2] += item[1]
            while item[2] >= 1.0:
                item[2] -= 1.0
                try:
                    next(item[0])
                except StopIteration:
                    live.remove(item)
                    break


def _chain(*gens):
    for gen in gens:
        yield from gen


KV_SPREAD = 2 * N_KV_HEADS * LANES


def _spread_kv_heads(t):
    low = lax.broadcasted_iota(jnp.int32, t.shape, 1) < HEAD_DIM
    tr = pltpu.roll(t, HEAD_DIM, axis=1)
    zero = jnp.zeros_like(t)
    blocks = [jnp.where(low, t, zero), jnp.where(low, zero, tr), jnp.where(low, tr, zero), jnp.where(low, zero, t)]
    return jnp.concatenate(blocks, axis=1).astype(BF16)


def _inproj_kernel(x_ref, g_ref, w_ref, q_ref, k_ref, vt_ref, u_ref, qm_ref):
    h = _rms(x_ref[...], g_ref[...]).astype(BF16)
    p = jnp.dot(h, w_ref[...], preferred_element_type=F32)
    scale = LOG2E * HEAD_DIM ** -0.5
    o1 = ATTN_WIDTH
    o2 = o1 + KV_WIDTH
    o3 = o2 + KV_WIDTH
    o4 = o3 + FOURIER_WIDTH
    q_ref[...] = (p[:, :o1] * scale).astype(BF16)
    k_ref[...] = _spread_kv_heads(p[:, o1:o2])
    vt_ref[...] = p[:, o2:o3].T.astype(BF16)
    u_ref[...] = p[:, o3:o4]
    qm_ref[...] = (p[:, o4:] * scale).astype(BF16)


def _inproj(x2, g, w, layer):
    t = x2.shape[0]
    tm = INPROJ_ROWS
    row = lambda i: (i, 0)
    fixed = lambda i: (0, 0)
    return pl.pallas_call(
        _inproj_kernel,
        grid=(t // tm,),
        in_specs=[
            pl.BlockSpec((tm, D_MODEL), row),
            pl.BlockSpec((1, D_MODEL), fixed),
            pl.BlockSpec((None, D_MODEL, IN_WIDTH), lambda i: (layer, 0, 0)),
        ],
        out_specs=[
            pl.BlockSpec((tm, ATTN_WIDTH), row),
            pl.BlockSpec((tm, KV_SPREAD), row),
            pl.BlockSpec((KV_WIDTH, tm), lambda i: (0, i)),
            pl.BlockSpec((tm, FOURIER_WIDTH), row),
            pl.BlockSpec((tm, MEM_WIDTH), row),
        ],
        out_shape=[
            jax.ShapeDtypeStruct((t, ATTN_WIDTH), BF16),
            jax.ShapeDtypeStruct((t, KV_SPREAD), BF16),
            jax.ShapeDtypeStruct((KV_WIDTH, t), BF16),
            jax.ShapeDtypeStruct((t, FOURIER_WIDTH), F32),
            jax.ShapeDtypeStruct((t, MEM_WIDTH), BF16),
        ],
        compiler_params=pltpu.CompilerParams(
            dimension_semantics=("arbitrary",), vmem_limit_bytes=VMEM_LIMIT),
        name="inproj",
    )(x2, g, w)


def _cmul_const(re, im, c, s):
    def near(a, b):
        return abs(a - b) < 1e-12
    if near(c, 1.0) and near(s, 0.0):
        return re, im
    if near(c, -1.0) and near(s, 0.0):
        return -re, -im
    if near(c, 0.0) and near(s, -1.0):
        return im, -re
    if near(c, 0.0) and near(s, 1.0):
        return -im, re
    return re * c - im * s, re * s + im * c


def _small_dft(xs):
    n = len(xs)
    if n == 1:
        return xs
    ev = _small_dft(xs[0::2])
    od = _small_dft(xs[1::2])
    out = [None] * n
    for k in range(n // 2):
        ang = -2.0 * math.pi * k / n
        tr, ti = _cmul_const(od[k][0], od[k][1], math.cos(ang), math.sin(ang))
        out[k] = (ev[k][0] + tr, ev[k][1] + ti)
        out[k + n // 2] = (ev[k][0] - tr, ev[k][1] - ti)
    return out


def _dft_stage1(u_ref, f_ref, zr_ref, zi_ref, p, *, n1, n2, m_chunk):
    xa = u_ref[pl.ds(2 * p, n1, stride=n2), :]
    xb = u_ref[pl.ds(2 * p + 1, n1, stride=n2), :]
    x2 = jnp.concatenate([xa, xb], axis=1).astype(BF16)
    for mc in range(n1 // m_chunk):
        rows = pl.ds(mc * m_chunk, m_chunk)
        zr = jnp.dot(f_ref[pl.ds(mc * m_chunk, m_chunk), :], x2, preferred_element_type=F32)
        zi = jnp.dot(f_ref[pl.ds(n1 + mc * m_chunk, m_chunk), :], x2, preferred_element_type=F32)
        zr_ref[2 * p, rows, :] = zr[:, :LANES]
        zr_ref[2 * p + 1, rows, :] = zr[:, LANES:]
        zi_ref[2 * p, rows, :] = zi[:, :LANES]
        zi_ref[2 * p + 1, rows, :] = zi[:, LANES:]


def _dft_stage2(twr_ref, twi_ref, zr_ref, zi_ref, r0, *, n2, r_chunk, wide_twiddles):
    rows = pl.ds(r0, r_chunk)
    if not wide_twiddles:
        twr = twr_ref[rows, :]
        twi = twi_ref[rows, :]
    xs = []
    for j in range(n2):
        re = zr_ref[j, rows, :]
        im = zi_ref[j, rows, :]
        if j > 0:
            if wide_twiddles:
                cr = twr_ref[j, rows, :]
                ci = twi_ref[j, rows, :]
            else:
                cr = twr[:, j:j + 1]
                ci = twi[:, j:j + 1]
            re, im = re * cr - im * ci, re * ci + im * cr
        xs.append((re, im))
    ys = _small_dft(xs)
    for j in range(n2):
        zr_ref[j, rows, :] = ys[j][0]
        zi_ref[j, rows, :] = ys[j][1]


def _dft_stage3(cm_ref, zr_ref, zi_ref, o_ref, j, lane0, *, n1, m_chunk, scale):
    for mc in range(n1 // m_chunk):
        rows = pl.ds(mc * m_chunk, m_chunk)
        xc = jnp.concatenate([zr_ref[j, rows, :], zi_ref[j, rows, :]], axis=1).astype(BF16)
        y = jnp.dot(xc, cm_ref[...], preferred_element_type=F32)
        o0 = pl.multiple_of(j * n1 + mc * m_chunk, m_chunk)
        o_ref[pl.ds(o0, m_chunk), lane0:lane0 + LANES] = y * scale


def _fourier_kernel(u_ref, f_ref, twr_ref, twi_ref, cm_ref, o_ref, zr_ref, zi_ref,
                    *, n1, n2, m_chunk, r_chunk, scale):
    def stage1(p, carry):
        _dft_stage1(u_ref, f_ref, zr_ref, zi_ref, p, n1=n1, n2=n2, m_chunk=m_chunk)
        return carry
    lax.fori_loop(0, n2 // 2, stage1, 0, unroll=2)

    def stage2(c, carry):
        _dft_stage2(twr_ref, twi_ref, zr_ref, zi_ref, pl.multiple_of(c * r_chunk, r_chunk),
                    n2=n2, r_chunk=r_chunk, wide_twiddles=False)
        return carry
    lax.fori_loop(0, n1 // r_chunk, stage2, 0, unroll=2)

    def stage3(j, carry):
        _dft_stage3(cm_ref, zr_ref, zi_ref, o_ref, j, 0, n1=n1, m_chunk=m_chunk, scale=scale)
        return carry
    lax.fori_loop(0, n2, stage3, 0, unroll=2)


def _fourier_pair_kernel(ua_ref, ub_ref, f_ref, twr_ref, twi_ref, cm_ref, o_ref, zr_ref, zi_ref,
                         *, n1, n2, m_chunk, r_chunk, scale):
    def slab(u_ref, sl):
        zr, zi = zr_ref.at[sl], zi_ref.at[sl]
        for p in range(n2 // 2):
            _dft_stage1(u_ref, f_ref, zr, zi, p, n1=n1, n2=n2, m_chunk=m_chunk)
            yield
        for c in range(n1 // r_chunk):
            _dft_stage2(twr_ref, twi_ref, zr, zi, c * r_chunk, n2=n2, r_chunk=r_chunk, wide_twiddles=True)
            if c % 4 == 3:
                yield
        for j in range(n2):
            _dft_stage3(cm_ref, zr, zi, o_ref, j, sl * LANES, n1=n1, m_chunk=m_chunk, scale=scale)
            yield

    def delay(n):
        for _ in range(n):
            yield

    _alternate([slab(ua_ref, 0), _chain(delay(n2 // 2), slab(ub_ref, 1))], (1.0, 1.0))


def _fourier_plan(s):
    n2 = 4 if s <= 4096 else 16
    return s // n2, n2


def _wide_twiddles(n2):
    return n2 <= 4


def _fourier_consts(s):
    n1, n2 = _fourier_plan(s)
    k = jnp.arange(n1, dtype=jnp.int32)
    ang = ((k[:, None] * k[None, :]) % n1).astype(F32) * (2.0 * math.pi / n1)
    fmat = jnp.concatenate([jnp.cos(ang), -jnp.sin(ang)], axis=0).astype(BF16)
    if _wide_twiddles(n2):
        j = jnp.arange(n2, dtype=jnp.int32)
        tang = ((j[:, None] * k[None, :]) % s).astype(F32) * (2.0 * math.pi / s)
        twr = jnp.broadcast_to(jnp.cos(tang)[:, :, None], (n2, n1, LANES))
        twi = jnp.broadcast_to(-jnp.sin(tang)[:, :, None], (n2, n1, LANES))
    else:
        j = jnp.arange(LANES, dtype=jnp.int32)
        tang = ((k[:, None] * j[None, :]) % s).astype(F32) * (2.0 * math.pi / s)
        live = j[None, :] < n2
        twr = jnp.where(live, jnp.cos(tang), 0.0)
        twi = jnp.where(live, -jnp.sin(tang), 0.0)
    c = jnp.arange(LANES, dtype=jnp.int32)
    same = (c[:, None] // FOURIER_GROUP_WIDTH) == (c[None, :] // FOURIER_GROUP_WIDTH)
    cang = ((c[:, None] * c[None, :]) % FOURIER_GROUP_WIDTH).astype(F32) * (2.0 * math.pi / FOURIER_GROUP_WIDTH)
    cmat = jnp.concatenate([jnp.where(same, jnp.cos(cang), 0.0),
                            jnp.where(same, jnp.sin(cang), 0.0)], axis=0).astype(BF16)
    return fmat, twr, twi, cmat


def _fourier(u, consts):
    b, s, _ = u.shape
    n1, n2 = _fourier_plan(s)
    fmat, twr, twi, cmat = consts
    params = dict(n1=n1, n2=n2, m_chunk=min(n1, 512), r_chunk=32, scale=float((s * FOURIER_GROUP_WIDTH) ** -0.5))
    once = pl.Buffered(1)
    cparams = lambda sem: pltpu.CompilerParams(dimension_semantics=sem, vmem_limit_bytes=VMEM_LIMIT)
    if _wide_twiddles(n2):
        fixed = lambda i: (0, 0)
        tw_spec = pl.BlockSpec((n2, n1, LANES), lambda i: (0, 0, 0), pipeline_mode=once)
        return pl.pallas_call(
            functools.partial(_fourier_pair_kernel, **params),
            grid=(b,),
            in_specs=[
                pl.BlockSpec((None, s, LANES), lambda i: (i, 0, 0)),
                pl.BlockSpec((None, s, LANES), lambda i: (i, 0, 1)),
                pl.BlockSpec((2 * n1, n1), fixed, pipeline_mode=once),
                tw_spec,
                tw_spec,
                pl.BlockSpec((2 * LANES, LANES), fixed, pipeline_mode=once),
            ],
            out_specs=pl.BlockSpec((None, s, FOURIER_WIDTH), lambda i: (i, 0, 0)),
            out_shape=jax.ShapeDtypeStruct((b, s, FOURIER_WIDTH), F32),
            scratch_shapes=[pltpu.VMEM((2, n2, n1, LANES), F32), pltpu.VMEM((2, n2, n1, LANES), F32)],
            compiler_params=cparams(("arbitrary",)),
            name="fourier",
        )(u, u, fmat, twr, twi, cmat)
    fixed = lambda i, j: (0, 0)
    return pl.pallas_call(
        functools.partial(_fourier_kernel, **params),
        grid=(b, FOURIER_WIDTH // LANES),
        in_specs=[
            pl.BlockSpec((None, s, LANES), lambda i, j: (i, 0, j), pipeline_mode=once),
            pl.BlockSpec((2 * n1, n1), fixed, pipeline_mode=once),
            pl.BlockSpec((n1, LANES), fixed, pipeline_mode=once),
            pl.BlockSpec((n1, LANES), fixed, pipeline_mode=once),
            pl.BlockSpec((2 * LANES, LANES), fixed, pipeline_mode=once),
        ],
        out_specs=pl.BlockSpec((None, s, LANES), lambda i, j: (i, 0, j), pipeline_mode=once),
        out_shape=jax.ShapeDtypeStruct((b, s, FOURIER_WIDTH), F32),
        scratch_shapes=[pltpu.VMEM((n2, n1, LANES), F32), pltpu.VMEM((n2, n1, LANES), F32)],
        compiler_params=cparams(("arbitrary", "arbitrary")),
        name="fourier",
    )(u, fmat, twr, twi, cmat)


def _memkv_kernel(m_ref, g_ref, w_ref, kbd_ref, vbt_ref):
    nb = m_ref.shape[0]
    h = _rms(m_ref[...].reshape(nb * N_MEM, D_MODEL), g_ref[...]).astype(BF16)
    kv = jnp.dot(h, w_ref[...], preferred_element_type=F32)
    ch_col = lax.broadcasted_iota(jnp.int32, (N_MEM, MEM_WIDTH), 1) // HEAD_DIM
    ch_row = lax.broadcasted_iota(jnp.int32, (MEM_WIDTH, N_MEM), 0) // HEAD_DIM
    for i in range(nb):
        km = kv[i * N_MEM:(i + 1) * N_MEM, :MEM_WIDTH]
        vt = kv[i * N_MEM:(i + 1) * N_MEM, MEM_WIDTH:].T
        for hd in range(N_MEM_HEADS):
            kbd_ref[i, hd * N_MEM:(hd + 1) * N_MEM, :] = jnp.where(ch_col == hd, km, 0.0).astype(BF16)
            vbt_ref[i, :, hd * N_MEM:(hd + 1) * N_MEM] = jnp.where(ch_row == hd, vt, 0.0).astype(BF16)


def _memkv(mem, g, w, layer):
    b = mem.shape[0]
    nb = MEMKV_SEQS if b % MEMKV_SEQS == 0 else 1
    fixed = lambda i: (0, 0)
    return pl.pallas_call(
        _memkv_kernel,
        grid=(b // nb,),
        in_specs=[
            pl.BlockSpec((nb, N_MEM, D_MODEL), lambda i: (i, 0, 0)),
            pl.BlockSpec((1, D_MODEL), fixed),
            pl.BlockSpec((None, D_MODEL, 2 * MEM_WIDTH), lambda i: (layer, 0, 0)),
        ],
        out_specs=[
            pl.BlockSpec((nb, N_MEM_HEADS * N_MEM, MEM_WIDTH), lambda i: (i, 0, 0)),
            pl.BlockSpec((nb, MEM_WIDTH, N_MEM_HEADS * N_MEM), lambda i: (i, 0, 0)),
        ],
        out_shape=[
            jax.ShapeDtypeStruct((b, N_MEM_HEADS * N_MEM, MEM_WIDTH), BF16),
            jax.ShapeDtypeStruct((b, MEM_WIDTH, N_MEM_HEADS * N_MEM), BF16),
        ],
        compiler_params=pltpu.CompilerParams(
            dimension_semantics=("arbitrary",), vmem_limit_bytes=VMEM_LIMIT),
        name="memkv",
    )(mem, g, w)


def _mixer_kernel(x_ref, q_ref, kp_ref, kc_ref, kn_ref, vp_ref, vc_ref, vn_ref, qm_ref, yf_ref,
                  kbd_ref, vbt_ref, sink_ref, ggrp_ref, wout_ref, gffn_ref, w1_ref, w2_ref, gfin_ref,
                  o_ref, ymix_ref, x1_ref, bias_ref,
                  *, tq, seq_len, final):
    nqb = tq // BLOCK
    nkb = nqb + 2
    jblock0 = pl.program_id(1) * nqb

    def k_rows(blk, kh, ab):
        lanes = slice((2 * kh + ab) * LANES, (2 * kh + ab + 1) * LANES)
        if blk == 0:
            return kp_ref[:, lanes]
        if blk == nkb - 1:
            return kn_ref[:, lanes]
        return kc_ref[pl.ds((blk - 1) * BLOCK, BLOCK), lanes]

    def vt_cols(blk, kh):
        ch = slice(kh * HEAD_DIM, (kh + 1) * HEAD_DIM)
        if blk == 0:
            return vp_ref[ch, :]
        if blk == nkb - 1:
            return vn_ref[ch, :]
        return vc_ref[ch, pl.ds((blk - 1) * BLOCK, BLOCK)]

    @pl.when((pl.program_id(0) == 0) & (pl.program_id(1) == 0))
    def _fill_bias():
        row = lax.broadcasted_iota(jnp.int32, (6 * BLOCK, 2 * BLOCK), 0)
        col = lax.broadcasted_iota(jnp.int32, (6 * BLOCK, 2 * BLOCK), 1)
        second_head = row >= 3 * BLOCK
        second_pair = col >= BLOCK
        kc = jnp.where(second_head, row - 3 * BLOCK, row)
        qa = jnp.where(second_pair, col - BLOCK, col)
        rel = kc - BLOCK - qa
        dist = jnp.abs(rel).astype(F32)
        in_window = jnp.abs(rel) <= WINDOW
        for variant, ok in enumerate((in_window, in_window & (kc >= BLOCK), in_window & (kc < 2 * BLOCK))):
            for kh in range(N_KV_HEADS):
                h0 = kh * GQA_GROUP
                slope = jnp.where(
                    second_pair,
                    jnp.where(second_head, 2.0 ** -(h0 + 4), 2.0 ** -(h0 + 3)),
                    jnp.where(second_head, 2.0 ** -(h0 + 2), 2.0 ** -(h0 + 1)))
                bias_ref[variant, kh] = jnp.where(ok, -LOG2E * slope * dist, NEG_INF)

    first_head = lax.broadcasted_iota(jnp.int32, (2 * HEAD_DIM, BLOCK), 0) < HEAD_DIM
    nt_dims = (((1,), (1,)), ((), ()))
    last_block = seq_len // BLOCK - 1
    zeros_v = jnp.zeros((HEAD_DIM, 3 * BLOCK), BF16)

    def unit_scores(qb, kh):
        rows = pl.ds(qb * BLOCK, BLOCK)
        jb = jblock0 + qb
        variant = jnp.where(jb == 0, 1, jnp.where(jb == last_block, 2, 0))
        kmat = jnp.concatenate([k_rows(qb + j, kh, ab) for ab in range(2) for j in range(3)], axis=0)
        vt = jnp.concatenate([vt_cols(qb + j, kh) for j in range(3)], axis=1)
        vmat = jnp.concatenate([jnp.concatenate([vt, zeros_v], axis=1),
                                jnp.concatenate([zeros_v, vt], axis=1)], axis=0)
        c0 = kh * GQA_GROUP * HEAD_DIM
        qpairs = jnp.concatenate([q_ref[rows, c0:c0 + LANES], q_ref[rows, c0 + LANES:c0 + 2 * LANES]], axis=0)
        st = lax.dot_general(kmat, qpairs, nt_dims, preferred_element_type=F32)
        return st + bias_ref[variant, kh], vmat

    def unit_finish(kh, st, vmat):
        pcols, scales = [], []
        for pair in range(2):
            ps, inv = [], []
            for e in range(2):
                t = st[e * 3 * BLOCK:(e + 1) * 3 * BLOCK, pair * BLOCK:(pair + 1) * BLOCK]
                sink = sink_ref[kh * GQA_GROUP + 2 * pair + e] * LOG2E
                m = jnp.maximum(jnp.max(t, axis=0, keepdims=True), sink)
                p = jnp.exp2(t - m)
                l = jnp.sum(p, axis=0, keepdims=True) + jnp.exp2(sink - m)
                ps.append(p.astype(BF16))
                inv.append(1.0 / l)
            pcols.append(jnp.concatenate(ps, axis=0))
            scales.append(jnp.where(first_head, inv[0], inv[1]))
        ot = jnp.dot(vmat, jnp.concatenate(pcols, axis=1), preferred_element_type=F32)
        ot = ot * jnp.concatenate(scales, axis=1)
        return [ot[:, pair * BLOCK:(pair + 1) * BLOCK].T for pair in range(2)]

    def mem_scores(qp):
        qm2 = qm_ref[pl.ds(qp * 2 * BLOCK, 2 * BLOCK), :]
        return lax.dot_general(kbd_ref[...], qm2, nt_dims, preferred_element_type=F32)

    def mem_finish(smt):
        pm, scales = [], []
        for hd in range(N_MEM_HEADS):
            t = smt[hd * N_MEM:(hd + 1) * N_MEM, :]
            m = jnp.max(t, axis=0, keepdims=True)
            p = jnp.exp2(t - m)
            inv = 1.0 / jnp.sum(p, axis=0, keepdims=True)
            pm.append(p.astype(BF16))
            scales.append(jnp.broadcast_to(inv, (HEAD_DIM, 2 * BLOCK)))
        omt = jnp.dot(vbt_ref[...], jnp.concatenate(pm, axis=0), preferred_element_type=F32)
        return omt * jnp.concatenate(scales, axis=0)

    def block_store(qb, outs, omt):
        rows = pl.ds(qb * BLOCK, BLOCK)
        y_attn = jnp.concatenate(outs, axis=1)
        y_mem = omt[:, (qb % 2) * BLOCK:(qb % 2 + 1) * BLOCK].T

        g = ggrp_ref[...]
        o1 = ATTN_WIDTH
        o2 = o1 + FOURIER_WIDTH
        ymix_ref[rows, :o1] = _rms(y_attn, g[:, :o1]).astype(BF16)
        ymix_ref[rows, o1:o2] = _rms(yf_ref[rows, :], g[:, o1:o2]).astype(BF16)
        ymix_ref[rows, o2:] = _rms(y_mem, g[:, o2:]).astype(BF16)

    units = [(qb, kh) for qb in range(nqb) for kh in range(N_KV_HEADS)]
    ready, mem, outs = {}, {}, {}
    for i in range(len(units) + LOOKAHEAD):
        if i < len(units):
            qb, kh = units[i]
            ready[(qb, kh)] = unit_scores(qb, kh)
            if kh == 0 and qb % 2 == 0:
                mem[qb // 2] = mem_scores(qb // 2)
        if i >= LOOKAHEAD:
            qb, kh = units[i - LOOKAHEAD]
            outs.setdefault(qb, []).extend(unit_finish(kh, *ready.pop((qb, kh))))
            if kh == 0 and qb % 2 == 0:
                mem[qb // 2] = mem_finish(mem[qb // 2])
            if kh == N_KV_HEADS - 1:
                block_store(qb, outs.pop(qb), mem[qb // 2])

    x1 = x_ref[...] + jnp.dot(ymix_ref[...], wout_ref[...], preferred_element_type=F32)
    x1_ref[...] = x1
    hn = _rms(x1, gffn_ref[...]).astype(BF16)
    acc = None
    for c in range(D_FF // FF_CHUNK):
        cols = slice(c * FF_CHUNK, (c + 1) * FF_CHUNK)
        hf = jnp.dot(hn, w1_ref[:, cols], preferred_element_type=F32)
        hf = jnp.square(jnp.maximum(hf, 0.0)).astype(BF16)
        part = jnp.dot(hf, w2_ref[cols, :], preferred_element_type=F32)
        acc = part if acc is None else acc + part
    y = x1_ref[...] + acc
    if final:
        y = _rms(y, gfin_ref[...])
    o_ref[...] = y


def _mixer(x, q, k, vt, qm, yf, kbd, vbt, sinks, g_grp, w_out, g_ffn, w1, w2, g_final, layer, final):
    b, s, _ = x.shape
    tq = ROW_TILE
    nqb = tq // BLOCK
    last_blk = s // BLOCK - 1
    tile = lambda i, j: (i, j, 0)
    prev = lambda i, j: (i, jnp.maximum(j * nqb - 1, 0), 0)
    nxt = lambda i, j: (i, jnp.minimum((j + 1) * nqb, last_blk), 0)
    per_b = lambda i, j: (i, 0, 0)
    this_layer = lambda i, j: (layer, 0, 0)
    fixed = lambda i, j: (0, 0)
    once = pl.Buffered(1)
    kern = functools.partial(_mixer_kernel, tq=tq, seq_len=s, final=final)
    return pl.pallas_call(
        kern,
        grid=(b, s // tq),
        in_specs=[
            pl.BlockSpec((None, tq, D_MODEL), tile),
            pl.BlockSpec((None, tq, ATTN_WIDTH), tile),
            pl.BlockSpec((None, BLOCK, KV_SPREAD), prev),
            pl.BlockSpec((None, tq, KV_SPREAD), tile),
            pl.BlockSpec((None, BLOCK, KV_SPREAD), nxt),
            pl.BlockSpec((KV_WIDTH, BLOCK), lambda i, j: (0, i * (s // BLOCK) + prev(i, j)[1])),
            pl.BlockSpec((KV_WIDTH, tq), lambda i, j: (0, i * (s // tq) + j)),
            pl.BlockSpec((KV_WIDTH, BLOCK), lambda i, j: (0, i * (s // BLOCK) + nxt(i, j)[1])),
            pl.BlockSpec((None, tq, MEM_WIDTH), tile),
            pl.BlockSpec((None, tq, FOURIER_WIDTH), tile),
            pl.BlockSpec((None, N_MEM_HEADS * N_MEM, MEM_WIDTH), per_b),
            pl.BlockSpec((None, MEM_WIDTH, N_MEM_HEADS * N_MEM), per_b),
            pl.BlockSpec(memory_space=pltpu.SMEM),
            pl.BlockSpec((1, D_MODEL), fixed),
            pl.BlockSpec((None, D_MODEL, D_MODEL), this_layer, pipeline_mode=once),
            pl.BlockSpec((1, D_MODEL), fixed),
            pl.BlockSpec((None, D_MODEL, D_FF), this_layer, pipeline_mode=once),
            pl.BlockSpec((None, D_FF, D_MODEL), this_layer, pipeline_mode=once),
            pl.BlockSpec((1, D_MODEL), fixed),
        ],
        out_specs=pl.BlockSpec((None, tq, D_MODEL), tile),
        out_shape=jax.ShapeDtypeStruct((b, s, D_MODEL), F32),
        scratch_shapes=[
            pltpu.VMEM((tq, D_MODEL), BF16),
            pltpu.VMEM((tq, D_MODEL), F32),
            pltpu.VMEM((3, N_KV_HEADS, 6 * BLOCK, 2 * BLOCK), F32),
        ],
        compiler_params=pltpu.CompilerParams(
            dimension_semantics=("arbitrary", "arbitrary"), vmem_limit_bytes=VMEM_LIMIT),
        name="mixer",
    )(x, q, k, k, k, vt, vt, vt, qm, yf, kbd, vbt, sinks, g_grp, w_out, g_ffn, w1, w2, g_final)


def _trunk(x, mem, g_mix, w_in, g_mem, w_mem_kv, sinks, g_grp, w_out, g_ffn, w_ff1, w_ff2, g_final):
    b, s, d = x.shape
    depth = g_mix.shape[0]
    consts = _fourier_consts(s)
    row = lambda a: a.reshape(1, -1)
    for l in range(depth):
        q, k, vt, u, qm = _inproj(x.reshape(b * s, d), row(g_mix[l]), w_in, l)
        shp = lambda a: a.reshape(b, s, a.shape[-1])
        yf = _fourier(shp(u), consts)
        kbd, vbt = _memkv(mem, row(g_mem[l]), w_mem_kv, l)
        x = _mixer(x, shp(q), shp(k), vt, shp(qm), yf, kbd, vbt, sinks[l], row(g_grp[l]), w_out,
                   row(g_ffn[l]), w_ff1, w_ff2, row(g_final), layer=l, final=(l == depth - 1))
    return x


def kernel(x_prompt, x_sample, mem_prompt, mem_sample, g_mix, w_in, g_mem, w_mem_kv, sinks, g_grp, w_out,
           g_ffn, w_ff1, w_ff2, g_final):
    weights = (g_mix, w_in.astype(BF16), g_mem, w_mem_kv.astype(BF16), sinks, g_grp, w_out.astype(BF16),
               g_ffn, w_ff1.astype(BF16), w_ff2.astype(BF16), g_final)
    y_prompt = _trunk(x_prompt, mem_prompt, *weights)
    y_sample = _trunk(x_sample, mem_sample, *weights)
    return (y_prompt, y_sample)
```
